```python
import math
import jax, jax.numpy as jnp
from jax import lax
import numpy as np

D_MODEL = 1024
BATCH = 8
SEQ = 2048
DEPTH = 2
DEC_BATCH = 128
DEC_SEQ = 4
PAST_LEN = 16384
PAGE_SIZE = 128

MIX_WIDTH = D_MODEL
RET_WIDTH = MIX_WIDTH // 2
CONV_WIDTH = MIX_WIDTH - RET_WIDTH
RET_HEADS = 4
RET_DK = RET_WIDTH // RET_HEADS
RET_DV = RET_WIDTH // RET_HEADS
RET_CHUNK = 128
ROPE_BASE = 10000.0
CONV_K = 31
IN_WIDTH = 4 * RET_WIDTH + 2 * CONV_WIDTH
PEER_HEADS = 8
N_KEYS = 128
N_EXPERTS = N_KEYS * N_KEYS
PEER_DQ = 256
PEER_TOPK = 16
PEER_BLOCK = 128
ALPHA = (2.0 * DEPTH) ** 0.25
BETA = (8.0 * DEPTH) ** -0.25
LN_EPS = 1e-5

kernel_name = "hymba_retnet_conformer_peer_step"


def layer_norm(x, g, b):
    xf = x.astype(jnp.float32)
    mu = jnp.mean(xf, axis=-1, keepdims=True)
    var = jnp.mean(jnp.square(xf - mu), axis=-1, keepdims=True)
    y = (xf - mu) * lax.rsqrt(var + LN_EPS)
    return (y * g.astype(jnp.float32) + b.astype(jnp.float32)).astype(x.dtype)


def head_norm(x):
    xf = x.astype(jnp.float32)
    mu = jnp.mean(xf, axis=-1, keepdims=True)
    var = jnp.mean(jnp.square(xf - mu), axis=-1, keepdims=True)
    return ((xf - mu) * lax.rsqrt(var + LN_EPS)).astype(x.dtype)


def rope_tables(pos, dtype):
    inv_freq = ROPE_BASE ** (-jnp.arange(0, RET_DK, 2, dtype=jnp.float32) / RET_DK)
    ang = pos.astype(jnp.float32)[:, None] * inv_freq[None, :]
    return jnp.cos(ang)[:, None, :].astype(dtype), jnp.sin(ang)[:, None, :].astype(dtype)


def apply_rope(t, cos, sin):
    t1, t2 = t[..., : RET_DK // 2], t[..., RET_DK // 2:]
    return jnp.concatenate([t1 * cos - t2 * sin, t1 * sin + t2 * cos], axis=-1)


def retention(q, k, v, r0, log_gamma):
    b, t, h, _ = q.shape
    c = min(t, RET_CHUNK)
    n = t // c
    dt = q.dtype
    qc = q.reshape(b, n, c, h, RET_DK)
    kc = k.reshape(b, n, c, h, RET_DK)
    vc = v.reshape(b, n, c, h, RET_DV)
    i = jnp.arange(c, dtype=jnp.float32)
    rel = i[:, None] - i[None, :]
    lg = log_gamma[:, None, None]
    dmask = jnp.where(rel >= 0, jnp.exp(jnp.maximum(rel, 0.0) * lg), 0.0).astype(dt)
    decay_in = jnp.exp((i[None, :] + 1.0) * log_gamma[:, None]).astype(dt)
    decay_out = jnp.exp((c - 1.0 - i[None, :]) * log_gamma[:, None]).astype(dt)
    chunk_decay = jnp.exp(c * log_gamma)[:, None, None].astype(dt)
    scores = jnp.einsum('bnihd,bnjhd->bnhij', qc, kc) * dmask
    o_intra = jnp.einsum('bnhij,bnjhe->bnihe', scores, vc)
    kv = jnp.einsum('bnjhd,bnjhe,hj->bnhde', kc, vc, decay_out)

    def step(r, kv_n):
        return chunk_decay * r + kv_n, r

    r_final, r_prev = lax.scan(step, r0, jnp.moveaxis(kv, 1, 0))
    r_prev = jnp.moveaxis(r_prev, 0, 1)
    o_cross = jnp.einsum('bnihd,bnhde,hi->bnihe', qc, r_prev, decay_in)
    return (o_intra + o_cross).reshape(b, t, h, RET_DV), r_final


def conformer_conv(ca, cg, buf, dw_kernel, dw_bias, ln_g, ln_b):
    u = ca * jax.nn.sigmoid(cg)
    xpad = jnp.concatenate([buf, u], axis=1)
    y = lax.conv_general_dilated(
        xpad, dw_kernel[:, None, :], window_strides=(1,), padding='VALID',
        dimension_numbers=('NWC', 'WIO', 'NWC'), feature_group_count=CONV_WIDTH) + dw_bias
    y = jax.nn.silu(layer_norm(y, ln_g, ln_b))
    return y, xpad[:, -(CONV_K - 1):]


def peer(x2d, w_query, sub_keys_1, sub_keys_2, peer_u, peer_v):
    t = x2d.shape[0]
    nb = -(-t // PEER_BLOCK)
    xp = jnp.pad(x2d, ((0, nb * PEER_BLOCK - t), (0, 0))).reshape(nb, PEER_BLOCK, D_MODEL)

    def block(xb):
        q = (xb @ w_query).reshape(PEER_BLOCK, PEER_HEADS, PEER_DQ)
        q1, q2 = q[..., : PEER_DQ // 2], q[..., PEER_DQ // 2:]
        s1 = jnp.einsum('phd,hkd->phk', q1, sub_keys_1)
        s2 = jnp.einsum('phd,hkd->phk', q2, sub_keys_2)
        v1, i1 = lax.top_k(s1, PEER_TOPK)
        v2, i2 = lax.top_k(s2, PEER_TOPK)
        cand = (v1[..., :, None] + v2[..., None, :]).reshape(PEER_BLOCK, PEER_HEADS, PEER_TOPK * PEER_TOPK)
        sv, ci = lax.top_k(cand, PEER_TOPK)
        e1 = jnp.take_along_axis(i1, ci // PEER_TOPK, axis=-1)
        e2 = jnp.take_along_axis(i2, ci % PEER_TOPK, axis=-1)
        idx = e1 * N_KEYS + e2
        gate = jax.nn.softmax(sv.astype(jnp.float32), axis=-1).astype(xb.dtype)
        u = jnp.take(peer_u, idx, axis=0)
        act = jax.nn.gelu(jnp.einsum('phkd,pd->phk', u, xb), approximate=False)
        vv = jnp.take(peer_v, idx, axis=0)
        return jnp.einsum('phk,phkd->pd', gate * act, vv)

    out = lax.map(block, xp)
    return out.reshape(nb * PEER_BLOCK, D_MODEL)[:t]


def decoder_layer(x, r0, buf, cos, sin, log_gamma, w_in, b_in, dw_kernel, dw_bias,
                  conv_ln_g, conv_ln_b, w_out, ln1_g, ln1_b, w_query, sub_keys_1,
                  sub_keys_2, peer_u, peer_v, ln2_g, ln2_b):
    b, t, _ = x.shape
    proj = jnp.einsum('btd,de->bte', x, w_in) + b_in
    q, k, v, g, ca, cg = jnp.split(proj, 6, axis=-1)
    q = apply_rope(q.reshape(b, t, RET_HEADS, RET_DK), cos, sin)
    k = apply_rope(k.reshape(b, t, RET_HEADS, RET_DK), cos, sin) * (RET_DK ** -0.5)
    v = v.reshape(b, t, RET_HEADS, RET_DV)
    o, r_new = retention(q, k, v, r0, log_gamma)
    ret_out = jax.nn.silu(g) * head_norm(o).reshape(b, t, RET_WIDTH)
    conv_out, buf_new = conformer_conv(ca, cg, buf, dw_kernel, dw_bias, conv_ln_g, conv_ln_b)
    mix = jnp.einsum('bte,ed->btd', jnp.concatenate([ret_out, conv_out], axis=-1), w_out)
    x = layer_norm(ALPHA * x + mix, ln1_g, ln1_b)
    ffn = peer(x.reshape(b * t, D_MODEL), w_query, sub_keys_1, sub_keys_2, peer_u, peer_v)
    x = layer_norm(ALPHA * x + ffn.reshape(b, t, D_MODEL), ln2_g, ln2_b)
    return x, r_new, buf_new


def setup_inputs(seed: int = 0) -> dict:
    key = jax.random.key(seed)
    ks = jax.random.split(key, 24)

    def nrm(k, shape, scale):
        return jax.random.normal(k, shape, jnp.float32) * scale

    s_in = D_MODEL ** -0.5
    w_in = jnp.concatenate([
        nrm(ks[0], (DEPTH, D_MODEL, RET_WIDTH), s_in),
        nrm(ks[1], (DEPTH, D_MODEL, RET_WIDTH), s_in),
        nrm(ks[2], (DEPTH, D_MODEL, RET_WIDTH), s_in * BETA),
        nrm(ks[3], (DEPTH, D_MODEL, RET_WIDTH), s_in),
        nrm(ks[4], (DEPTH, D_MODEL, CONV_WIDTH), s_in),
        nrm(ks[5], (DEPTH, D_MODEL, CONV_WIDTH), s_in),
    ], axis=-1)
    return {
        "x_prompt": nrm(ks[6], (BATCH, SEQ, D_MODEL), 1.0),
        "x_sample": nrm(ks[7], (DEC_BATCH, DEC_SEQ, D_MODEL), 1.0),
        "state_retention": nrm(ks[8], (DEPTH, DEC_BATCH, RET_HEADS, RET_DK, RET_DV), 0.3),
        "cache_conv": nrm(ks[9], (DEPTH, DEC_BATCH, CONV_K - 1, CONV_WIDTH), 0.5),
        "w_in": w_in,
        "b_in": nrm(ks[10], (DEPTH, IN_WIDTH), 0.02),
        "dw_kernel": nrm(ks[11], (DEPTH, CONV_K, CONV_WIDTH), CONV_K ** -0.5),
        "dw_bias": nrm(ks[12], (DEPTH, CONV_WIDTH), 0.02),
        "conv_ln_g": 1.0 + nrm(ks[13], (DEPTH, CONV_WIDTH), 0.02),
        "conv_ln_b": nrm(ks[14], (DEPTH, CONV_WIDTH), 0.02),
        "w_out": nrm(ks[15], (DEPTH, MIX_WIDTH, D_MODEL), MIX_WIDTH ** -0.5 * BETA),
        "ln1_g": 1.0 + nrm(ks[16], (DEPTH, D_MODEL), 0.02),
        "ln1_b": nrm(ks[17], (DEPTH, D_MODEL), 0.02),
        "w_query": nrm(ks[18], (DEPTH, D_MODEL, PEER_HEADS * PEER_DQ), s_in),
        "sub_keys_1": nrm(ks[19], (DEPTH, PEER_HEADS, N_KEYS, PEER_DQ // 2), (PEER_DQ // 2) ** -0.5),
        "sub_keys_2": nrm(ks[20], (DEPTH, PEER_HEADS, N_KEYS, PEER_DQ // 2), (PEER_DQ // 2) ** -0.5),
        "peer_u": nrm(ks[21], (DEPTH, N_EXPERTS, D_MODEL), s_in),
        "peer_v": nrm(ks[22], (DEPTH, N_EXPERTS, D_MODEL), BETA * PEER_HEADS ** -0.5),
        "ln2_g": 1.0 + nrm(ks[23], (DEPTH, D_MODEL), 0.02),
        "ln2_b": nrm(jax.random.fold_in(key, 99), (DEPTH, D_MODEL), 0.02),
    }


def reference(x_prompt, x_sample, state_retention, cache_conv, w_in, b_in, dw_kernel,
              dw_bias, conv_ln_g, conv_ln_b, w_out, ln1_g, ln1_b, w_query, sub_keys_1,
              sub_keys_2, peer_u, peer_v, ln2_g, ln2_b):
    dt = x_prompt.dtype
    log_gamma = jnp.log(1.0 - 2.0 ** (-5.0 - jnp.arange(RET_HEADS, dtype=jnp.float32)))
    cos_p, sin_p = rope_tables(jnp.arange(x_prompt.shape[1]), dt)
    cos_s, sin_s = rope_tables(PAST_LEN + jnp.arange(x_sample.shape[1]), dt)
    bp = x_prompt.shape[0]
    yp, ys = x_prompt, x_sample
    ret_p, conv_p, ret_s, conv_s = [], [], [], []
    for l in range(DEPTH):
        params = (w_in[l], b_in[l], dw_kernel[l], dw_bias[l], conv_ln_g[l], conv_ln_b[l],
                  w_out[l], ln1_g[l], ln1_b[l], w_query[l], sub_keys_1[l], sub_keys_2[l],
                  peer_u[l], peer_v[l], ln2_g[l], ln2_b[l])
        r0 = jnp.zeros((bp, RET_HEADS, RET_DK, RET_DV), dt)
        buf0 = jnp.zeros((bp, CONV_K - 1, CONV_WIDTH), dt)
        yp, rp, cp = decoder_layer(yp, r0, buf0, cos_p, sin_p, log_gamma, *params)
        ys, rs, cs = decoder_layer(ys, state_retention[l], cache_conv[l], cos_s, sin_s, log_gamma, *params)
        ret_p.append(rp)
        conv_p.append(cp)
        ret_s.append(rs)
        conv_s.append(cs)
    return (yp, ys, jnp.stack(ret_p), jnp.stack(conv_p), jnp.stack(ret_s), jnp.stack(conv_s))
```

```python
import functools
import math

import jax
import jax.numpy as jnp
from jax import lax
from jax.experimental import pallas as pl
from jax.experimental.pallas import tpu as pltpu

D_MODEL = 1024
DEPTH = 2
PAST_LEN = 16384
RET_WIDTH = 512
CONV_WIDTH = 512
RET_HEADS = 4
RET_DK = 128
RET_CHUNK = 128
ROPE_BASE = 10000.0
CONV_K = 31
IN_WIDTH = 4 * RET_WIDTH + 2 * CONV_WIDTH
PEER_HEADS = 8
N_KEYS = 128
PEER_DQ = 256
PEER_TOPK = 16
ALPHA = (2.0 * DEPTH) ** 0.25
LN_EPS = 1e-5

LANES = 128
HIST_ROWS = 32
VMEM_LIMIT = 56 * 1024 * 1024

BF16 = jnp.bfloat16
F32 = jnp.float32


def _dot(a, b):
    return jnp.dot(a.astype(BF16), b.astype(BF16), preferred_element_type=F32)


def _dot_nt(a, b):
    return lax.dot_general(a.astype(BF16), b.astype(BF16), (((1,), (1,)), ((), ())),
                           preferred_element_type=F32)


def _norm(x):
    mu = jnp.mean(x, axis=-1, keepdims=True)
    xc = x - mu
    var = jnp.mean(xc * xc, axis=-1, keepdims=True)
    return xc * lax.rsqrt(var + LN_EPS)


def _silu(x):
    return x * jax.nn.sigmoid(x)


def _rope(t, cos2, sin2):
    return t * cos2 + pltpu.roll(t, RET_DK // 2, 1) * sin2


def _mixer_prompt_kernel(x_ref, win_ref, bin_ref, cos_ref, sin_ref, dmask_ref, din_ref, dout_ref,
                         cdec_ref, dw_ref, dwb_ref, clg_ref, clb_ref, wout_ref, l1g_ref, l1b_ref,
                         y_ref, r_ref, cb_ref,
                         rstate, ubuf, mixbuf, *, tt):
    t = pl.program_id(1)

    @pl.when(t == 0)
    def _():
        rstate[...] = jnp.zeros_like(rstate)
        ubuf[0:HIST_ROWS, :] = jnp.zeros((HIST_ROWS, CONV_WIDTH), F32)

    x = x_ref[0]
    xb = x.astype(BF16)

    def proj(seg):
        lo = seg * RET_WIDTH
        return (jnp.dot(xb, win_ref[:, lo:lo + RET_WIDTH], preferred_element_type=F32)
                + bin_ref[:, lo:lo + RET_WIDTH])

    q, k, v, g = proj(0), proj(1), proj(2), proj(3)
    u = proj(4) * jax.nn.sigmoid(proj(5))

    for h in range(RET_HEADS):
        hs = slice(h * RET_DK, (h + 1) * RET_DK)
        dmask, din, dout, cdec = dmask_ref[h], din_ref[h], dout_ref[h], cdec_ref[h]
        for n in range(tt // RET_CHUNK):
            rs = slice(n * RET_CHUNK, (n + 1) * RET_CHUNK)
            cos2, sin2 = cos_ref[rs, :], sin_ref[rs, :]
            qc = _rope(q[rs, hs], cos2, sin2)
            kc = _rope(k[rs, hs], cos2, sin2) * (RET_DK ** -0.5)
            vc = v[rs, hs]
            r_prev = rstate[h]
            scores = _dot_nt(qc, kc) * dmask
            o = _dot(scores, vc) + _dot(qc * din, r_prev)
            kv = _dot((kc * dout).T, vc)
            rstate[h] = cdec * r_prev + kv
            mixbuf[rs, hs] = (_silu(g[rs, hs]) * _norm(o)).astype(BF16)

    ubuf[HIST_ROWS:HIST_ROWS + tt, :] = u
    cr = 64
    for r0 in range(0, tt, cr):
        acc = jnp.zeros((cr, CONV_WIDTH), F32) + dwb_ref[...]
        for kk in range(CONV_K):
            s = HIST_ROWS - (CONV_K - 1) + r0 + kk
            acc = acc + ubuf[s:s + cr, :] * dw_ref[kk:kk + 1, :]
        yc = _silu(_norm(acc) * clg_ref[...] + clb_ref[...])
        mixbuf[r0:r0 + cr, RET_WIDTH:] = yc.astype(BF16)
    ubuf[0:HIST_ROWS, :] = ubuf[tt:tt + HIST_ROWS, :]

    mix = jnp.dot(mixbuf[...], wout_ref[...], preferred_element_type=F32)
    y_ref[0] = _norm(ALPHA * x + mix) * l1g_ref[...] + l1b_ref[...]

    @pl.when(t == pl.num_programs(1) - 1)
    def _():
        r_ref[0] = rstate[...]
        cb_ref[0] = ubuf[0:HIST_ROWS, :]


def _mixer_prompt(x, win, b_in, cos2, sin2, dmask, din, dout, cdec, dw, dwb, clg, clb, wout, l1g, l1b):
    b, t, d = x.shape
    tt = min(512, t)
    assert t % tt == 0 and tt % RET_CHUNK == 0
    full = lambda a: pl.BlockSpec(a.shape, lambda i, j: (0,) * a.ndim)
    args = (x, win, b_in, cos2, sin2, dmask, din, dout, cdec, dw, dwb, clg, clb, wout, l1g, l1b)
    in_specs = [pl.BlockSpec((1, tt, d), lambda i, j: (i, j, 0)), full(win), full(b_in),
                pl.BlockSpec((tt, RET_DK), lambda i, j: (j, 0)),
                pl.BlockSpec((tt, RET_DK), lambda i, j: (j, 0))] + [full(a) for a in args[5:]]
    return pl.pallas_call(
        functools.partial(_mixer_prompt_kernel, tt=tt),
        grid=(b, t // tt),
        in_specs=in_specs,
        out_specs=[pl.BlockSpec((1, tt, d), lambda i, j: (i, j, 0)),
                   pl.BlockSpec((1, RET_HEADS, RET_DK, RET_DK), lambda i, j: (i, 0, 0, 0)),
                   pl.BlockSpec((1, HIST_ROWS, CONV_WIDTH), lambda i, j: (i, 0, 0))],
        out_shape=[jax.ShapeDtypeStruct((b, t, d), F32),
                   jax.ShapeDtypeStruct((b, RET_HEADS, RET_DK, RET_DK), F32),
                   jax.ShapeDtypeStruct((b, HIST_ROWS, CONV_WIDTH), F32)],
        scratch_shapes=[pltpu.VMEM((RET_HEADS, RET_DK, RET_DK), F32),
                        pltpu.VMEM((tt + HIST_ROWS, CONV_WIDTH), F32),
                        pltpu.VMEM((tt, D_MODEL), BF16)],
        compiler_params=pltpu.CompilerParams(dimension_semantics=("arbitrary", "arbitrary"),
                                             vmem_limit_bytes=VMEM_LIMIT),
        name="mixer_prompt",
    )(*args)


def _mixer_sample_kernel(x_ref, r0_ref, cache_ref, win_ref, bin_ref, cos_ref, sin_ref, stab_ref,
                         dw_ref, dwb_ref, clg_ref, clb_ref, wout_ref, l1g_ref, l1b_ref,
                         y_ref, r_ref, cn_ref,
                         qbuf, kbuf, mixbuf, *, nb, ns):
    rows = ns * nb
    x = x_ref[...].reshape(rows, D_MODEL)
    xb = x.astype(BF16)

    def proj(seg):
        lo = seg * RET_WIDTH
        return (jnp.dot(xb, win_ref[:, lo:lo + RET_WIDTH], preferred_element_type=F32)
                + bin_ref[:, lo:lo + RET_WIDTH])

    q, k, v, g = proj(0), proj(1), proj(2), proj(3)
    u = proj(4) * jax.nn.sigmoid(proj(5))
    row_b = lax.broadcasted_iota(jnp.int32, (rows, RET_DK), 0) % nb

    for h in range(RET_HEADS):
        hs = slice(h * RET_DK, (h + 1) * RET_DK)
        qs, ks, vs = [], [], []
        for j in range(ns):
            rs = slice(j * nb, (j + 1) * nb)
            cos2, sin2 = cos_ref[j:j + 1, :], sin_ref[j:j + 1, :]
            qj = _rope(q[rs, hs], cos2, sin2)
            kj = _rope(k[rs, hs], cos2, sin2) * (RET_DK ** -0.5)
            qs.append(qj)
            ks.append(kj)
            vs.append(v[rs, hs])
            qbuf[rs, :] = qj * stab_ref[h, j:j + 1, :]
            kbuf[rs, :] = kj * stab_ref[h, 4 + j:5 + j, :]
        qd = qbuf[...]
        kdt = kbuf[...].T
        vh = v[:, hs]
        cdec = stab_ref[h, 8:9, :]
        o_cross = jnp.zeros((rows, RET_DK), F32)
        for b in range(nb):
            sel = row_b == b
            r_prev = r0_ref[b, h]
            o_cross = o_cross + _dot(jnp.where(sel, qd, 0.0), r_prev)
            r_ref[b, h] = cdec * r_prev + _dot(kdt, jnp.where(sel, vh, 0.0))
        for i in range(ns):
            rs = slice(i * nb, (i + 1) * nb)
            o = o_cross[rs, :]
            for j in range(i + 1):
                sc = jnp.sum(qs[i] * ks[j], axis=-1, keepdims=True)
                o = o + (sc * stab_ref[h, 16 + 4 * i + j:17 + 4 * i + j, :]) * vs[j]
            mixbuf[rs, hs] = (_silu(g[rs, hs]) * _norm(o)).astype(BF16)

    nc = CONV_K - 1
    us = [u[j * nb:(j + 1) * nb, :] for j in range(ns)]
    for t in range(ns):
        acc = jnp.zeros((nb, CONV_WIDTH), F32) + dwb_ref[...]
        for r in range(t, nc):
            acc = acc + cache_ref[:, r * CONV_WIDTH:(r + 1) * CONV_WIDTH] * dw_ref[r - t:r - t + 1, :]
        for j in range(t + 1):
            kk = nc - t + j
            acc = acc + us[j] * dw_ref[kk:kk + 1, :]
        yc = _silu(_norm(acc) * clg_ref[...] + clb_ref[...])
        mixbuf[t * nb:(t + 1) * nb, RET_WIDTH:] = yc.astype(BF16)
    cn_ref[:, 0:(nc - ns) * CONV_WIDTH] = cache_ref[:, ns * CONV_WIDTH:nc * CONV_WIDTH]
    for j in range(ns):
        lo = (nc - ns + j) * CONV_WIDTH
        cn_ref[:, lo:lo + CONV_WIDTH] = us[j]

    mix = jnp.dot(mixbuf[...], wout_ref[...], preferred_element_type=F32)
    y = _norm(ALPHA * x + mix) * l1g_ref[...] + l1b_ref[...]
    y_ref[...] = y.reshape(ns, nb, D_MODEL)


def _mixer_sample(x_tm, r0, cache_flat, win, b_in, cos2, sin2, stab, dw, dwb, clg, clb, wout, l1g, l1b):
    ns, nbatch, d = x_tm.shape
    nb = 16
    assert nbatch % nb == 0 and ns == 4
    full = lambda a: pl.BlockSpec(a.shape, lambda i: (0,) * a.ndim)
    args = (x_tm, r0, cache_flat, win, b_in, cos2, sin2, stab, dw, dwb, clg, clb, wout, l1g, l1b)
    cw = cache_flat.shape[1]
    in_specs = [pl.BlockSpec((ns, nb, d), lambda i: (0, i, 0)),
                pl.BlockSpec((nb, RET_HEADS, RET_DK, RET_DK), lambda i: (i, 0, 0, 0)),
                pl.BlockSpec((nb, cw), lambda i: (i, 0))] + [full(a) for a in args[3:]]
    return pl.pallas_call(
        functools.partial(_mixer_sample_kernel, nb=nb, ns=ns),
        grid=(nbatch // nb,),
        in_specs=in_specs,
        out_specs=[pl.BlockSpec((ns, nb, d), lambda i: (0, i, 0)),
                   pl.BlockSpec((nb, RET_HEADS, RET_DK, RET_DK), lambda i: (i, 0, 0, 0)),
                   pl.BlockSpec((nb, cw), lambda i: (i, 0))],
        out_shape=[jax.ShapeDtypeStruct((ns, nbatch, d), F32),
                   jax.ShapeDtypeStruct(r0.shape, F32),
                   jax.ShapeDtypeStruct(cache_flat.shape, F32)],
        scratch_shapes=[pltpu.VMEM((ns * nb, RET_DK), F32),
                        pltpu.VMEM((ns * nb, RET_DK), F32),
                        pltpu.VMEM((ns * nb, D_MODEL), BF16)],
        compiler_params=pltpu.CompilerParams(dimension_semantics=("arbitrary",),
                                             vmem_limit_bytes=VMEM_LIMIT),
        name="mixer_sample",
    )(*args)


PEER_ROWS = 16
NEG_INF = float("-inf")


def _top16(s, vs_ref):
    work = s
    for r in range(PEER_TOPK):
        m = jnp.max(work, axis=0, keepdims=True)
        vs_ref[r:r + 1, :] = m
        if r + 1 < PEER_TOPK:
            work = jnp.where(work == m, NEG_INF, work)


def _peer_kernel(x_ref, wq_ref, sk1_ref, sk2_ref, u_ref, vt_ref, l2g_ref, l2b_ref, y_ref,
                 xb_ref, s1_ref, s2_ref, p2_ref, w1_ref, tau_ref, v1_ref, v2_ref, cand_ref,
                 h_ref, ga_ref, acc_ref, *, tm, te):
    j = pl.program_id(1)
    e1_per_blk = te // N_KEYS

    @pl.when(j == 0)
    def _():
        xb = x_ref[...].astype(BF16)
        xb_ref[...] = xb
        q = jnp.dot(xb, wq_ref[...], preferred_element_type=F32)
        acc_ref[...] = jnp.zeros_like(acc_ref)
        half = PEER_DQ // 2
        for h in range(PEER_HEADS):
            s1 = _dot_nt(sk1_ref[h], q[:, h * PEER_DQ:h * PEER_DQ + half])
            s2 = _dot_nt(sk2_ref[h], q[:, h * PEER_DQ + half:(h + 1) * PEER_DQ])
            s1_ref[h] = s1
            s2_ref[h] = s2
            _top16(s1, v1_ref)
            _top16(s2, v2_ref)
            cand_ref[0:PEER_TOPK, :] = v1_ref[0:1, :] + v2_ref[...]
            for i in range(1, PEER_TOPK):
                lo = PEER_TOPK + 8 * (i - 1)
                cand_ref[lo:lo + 8, :] = v1_ref[i:i + 1, :] + v2_ref[0:8, :]
            work = cand_ref[...]
            top = v1_ref[0:1, :] + v2_ref[0:1, :]
            z = jnp.zeros_like(top)
            for r in range(PEER_TOPK):
                m = jnp.max(work, axis=0, keepdims=True)
                z = z + jnp.exp(m - top)
                if r + 1 < PEER_TOPK:
                    work = jnp.where(work == m, NEG_INF, work)
            tau_ref[h:h + 1, :] = m
            p2_ref[h] = jnp.exp(s2 - v2_ref[0:1, :])
            w1_ref[h] = jnp.exp(s1 - v1_ref[0:1, :]) / z

    h_ref[...] = _dot_nt(u_ref[...], xb_ref[...])

    def body(it, carry):
        r0 = pl.multiple_of(it * PEER_ROWS, PEER_ROWS)
        e1 = j * e1_per_blk + it // (N_KEYS // PEER_ROWS)
        k0 = pl.multiple_of((it % (N_KEYS // PEER_ROWS)) * PEER_ROWS, PEER_ROWS)
        gate = jnp.zeros((PEER_ROWS, tm), F32)
        for h in range(PEER_HEADS):
            a = s1_ref[h, pl.ds(e1, 1), :]
            w = w1_ref[h, pl.ds(e1, 1), :]
            s2 = s2_ref[h, pl.ds(k0, PEER_ROWS), :]
            p2 = p2_ref[h, pl.ds(k0, PEER_ROWS), :]
            sel = (s2 + a) >= tau_ref[h:h + 1, :]
            gate = gate + jnp.where(sel, p2 * w, 0.0)
        hv = h_ref[pl.ds(r0, PEER_ROWS), :]
        act = 0.5 * hv * (1.0 + lax.erf(hv * (2.0 ** -0.5)))
        ga_ref[pl.ds(r0, PEER_ROWS), :] = (gate * act).astype(BF16)
        return carry

    lax.fori_loop(0, te // PEER_ROWS, body, 0)
    acc_ref[...] += jnp.dot(vt_ref[...], ga_ref[...], preferred_element_type=F32)

    @pl.when(j == pl.num_programs(1) - 1)
    def _():
        y = ALPHA * x_ref[...] + acc_ref[...].T
        y_ref[...] = _norm(y) * l2g_ref[...] + l2b_ref[...]


def _peer(x, wq, sk1, sk2, u_b, vt_b, l2g, l2b):
    n, d = x.shape
    ne = u_b.shape[0]
    tm = min(512, n)
    te = 512
    assert n % tm == 0 and ne % te == 0 and te % N_KEYS == 0
    full = lambda a: pl.BlockSpec(a.shape, lambda i, j: (0,) * a.ndim)
    hk = (PEER_HEADS, N_KEYS, tm)
    ncand = PEER_TOPK + 8 * (PEER_TOPK - 1)
    return pl.pallas_call(
        functools.partial(_peer_kernel, tm=tm, te=te),
        grid=(n // tm, ne // te),
        in_specs=[pl.BlockSpec((tm, d), lambda i, j: (i, 0)), full(wq), full(sk1), full(sk2),
                  pl.BlockSpec((te, d), lambda i, j: (j, 0)),
                  pl.BlockSpec((d, te), lambda i, j: (0, j)), full(l2g), full(l2b)],
        out_specs=pl.BlockSpec((tm, d), lambda i, j: (i, 0)),
        out_shape=jax.ShapeDtypeStruct((n, d), F32),
        scratch_shapes=[pltpu.VMEM((tm, d), BF16),
                        pltpu.VMEM(hk, F32), pltpu.VMEM(hk, F32), pltpu.VMEM(hk, F32), pltpu.VMEM(hk, F32),
                        pltpu.VMEM((PEER_HEADS, tm), F32),
                        pltpu.VMEM((PEER_TOPK, tm), F32), pltpu.VMEM((PEER_TOPK, tm), F32),
                        pltpu.VMEM((ncand, tm), F32),
                        pltpu.VMEM((te, tm), F32), pltpu.VMEM((te, tm), BF16),
                        pltpu.VMEM((d, tm), F32)],
        compiler_params=pltpu.CompilerParams(dimension_semantics=("arbitrary", "arbitrary"),
                                             vmem_limit_bytes=VMEM_LIMIT),
        name="peer",
    )(x, wq, sk1, sk2, u_b, vt_b, l2g, l2b)


def _rope_tables(pos):
    inv_freq = ROPE_BASE ** (-jnp.arange(0, RET_DK, 2, dtype=F32) / RET_DK)
    ang = pos.astype(F32)[:, None] * inv_freq[None, :]
    cos, sin = jnp.cos(ang), jnp.sin(ang)
    return jnp.concatenate([cos, cos], axis=-1), jnp.concatenate([-sin, sin], axis=-1)


def _decay_tables(log_gamma, c):
    i = jnp.arange(c, dtype=F32)
    rel = i[:, None] - i[None, :]
    lg = log_gamma[:, None, None]
    dmask = jnp.where(rel >= 0, jnp.exp(jnp.maximum(rel, 0.0) * lg), 0.0)
    decay_in = jnp.exp((i[None, :] + 1.0) * log_gamma[:, None])
    decay_out = jnp.exp((c - 1.0 - i[None, :]) * log_gamma[:, None])
    chunk_decay = jnp.exp(c * log_gamma)
    return dmask, decay_in, decay_out, chunk_decay


def kernel(x_prompt, x_sample, state_retention, cache_conv, w_in, b_in, dw_kernel, dw_bias, conv_ln_g,
           conv_ln_b, w_out, ln1_g, ln1_b, w_query, sub_keys_1, sub_keys_2, peer_u, peer_v, ln2_g, ln2_b):
    bp, tp, d = x_prompt.shape
    bs, ns, _ = x_sample.shape
    log_gamma = jnp.log(1.0 - 2.0 ** (-5.0 - jnp.arange(RET_HEADS, dtype=F32)))
    cos_p, sin_p = _rope_tables(jnp.arange(tp))
    cos_s, sin_s = _rope_tables(PAST_LEN + jnp.arange(ns))

    cp = min(tp, RET_CHUNK)
    dmask, d_in, d_out, c_dec = _decay_tables(log_gamma, cp)
    rep = lambda a: jnp.broadcast_to(a[..., None], a.shape + (LANES,))
    din_t, dout_t = rep(d_in), rep(d_out)
    cdec_t = jnp.broadcast_to(c_dec[:, None, None], (RET_HEADS, cp, LANES))

    dm4, di4, do4, cd4 = _decay_tables(log_gamma, ns)
    stab = jnp.concatenate([di4, do4, cd4[:, None], jnp.zeros((RET_HEADS, 7), F32),
                            dm4.reshape(RET_HEADS, ns * ns)], axis=1)
    stab = rep(stab)

    yp = x_prompt
    ys = jnp.transpose(x_sample, (1, 0, 2))
    row = lambda a: a.reshape(1, -1)
    ret_p, conv_p, ret_s, conv_s = [], [], [], []
    for l in range(DEPTH):
        win, wout, wq = w_in[l].astype(BF16), w_out[l].astype(BF16), w_query[l].astype(BF16)
        mix_w = (row(b_in[l]),)
        conv_w = (dw_kernel[l], row(dw_bias[l]), row(conv_ln_g[l]), row(conv_ln_b[l]), wout,
                  row(ln1_g[l]), row(ln1_b[l]))
        peer_w = (wq, sub_keys_1[l].astype(BF16), sub_keys_2[l].astype(BF16),
                  peer_u[l].astype(BF16), peer_v[l].T.astype(BF16), row(ln2_g[l]), row(ln2_b[l]))

        xp, rp, cbp = _mixer_prompt(yp, win, *mix_w, cos_p, sin_p, dmask, din_t, dout_t, cdec_t, *conv_w)
        xs, rs, cs = _mixer_sample(ys, state_retention[l], cache_conv[l].reshape(bs, -1), win, *mix_w,
                                   cos_s, sin_s, stab, *conv_w)
        yp = _peer(xp.reshape(bp * tp, d), *peer_w).reshape(bp, tp, d)
        ys = _peer(xs.reshape(ns * bs, d), *peer_w).reshape(ns, bs, d)
        ret_p.append(rp)
        conv_p.append(cbp[:, HIST_ROWS - (CONV_K - 1):])
        ret_s.append(rs)
        conv_s.append(cs.reshape(bs, CONV_K - 1, CONV_WIDTH))
    return (yp, jnp.transpose(ys, (1, 0, 2)), jnp.stack(ret_p), jnp.stack(conv_p),
            jnp.stack(ret_s), jnp.stack(conv_s))
```

```python
import functools
import math

import jax
import jax.numpy as jnp
from jax import lax
from jax.experimental import pallas as pl
from jax.experimental.pallas import tpu as pltpu

D_MODEL = 1024
DEPTH = 2
PAST_LEN = 16384
RET_WIDTH = 512
CONV_WIDTH = 512
RET_HEADS = 4
RET_DK = 128
RET_CHUNK = 128
ROPE_BASE = 10000.0
CONV_K = 31
IN_WIDTH = 4 * RET_WIDTH + 2 * CONV_WIDTH
PEER_HEADS = 8
N_KEYS = 128
PEER_DQ = 256
PEER_TOPK = 16
ALPHA = (2.0 * DEPTH) ** 0.25
LN_EPS = 1e-5

LANES = 128
HIST_ROWS = 32
VMEM_LIMIT = 56 * 1024 * 1024

BF16 = jnp.bfloat16
F32 = jnp.float32


def _dot(a, b):
    return jnp.dot(a.astype(BF16), b.astype(BF16), preferred_element_type=F32)


def _dot_nt(a, b):
    return lax.dot_general(a.astype(BF16), b.astype(BF16), (((1,), (1,)), ((), ())),
                           preferred_element_type=F32)


def _norm(x):
    mu = jnp.mean(x, axis=-1, keepdims=True)
    xc = x - mu
    var = jnp.mean(xc * xc, axis=-1, keepdims=True)
    return xc * lax.rsqrt(var + LN_EPS)


def _silu(x):
    return x * jax.nn.sigmoid(x)


def _rope(t, cos2, sin2):
    return t * cos2 + pltpu.roll(t, RET_DK // 2, 1) * sin2


def _mixer_prompt_kernel(x_ref, win_ref, bin_ref, cos_ref, sin_ref, dmask_ref, din_ref, dout_ref,
                         cdec_ref, dw_ref, dwb_ref, clg_ref, clb_ref, wout_ref, l1g_ref, l1b_ref,
                         y_ref, r_ref, cb_ref,
                         rstate, ubuf, mixbuf, *, tt):
    t = pl.program_id(1)

    @pl.when(t == 0)
    def _():
        rstate[...] = jnp.zeros_like(rstate)
        ubuf[0:HIST_ROWS, :] = jnp.zeros((HIST_ROWS, CONV_WIDTH), F32)

    x = x_ref[0]
    xb = x.astype(BF16)

    def proj(seg):
        lo = seg * RET_WIDTH
        return (jnp.dot(xb, win_ref[:, lo:lo + RET_WIDTH], preferred_element_type=F32)
                + bin_ref[:, lo:lo + RET_WIDTH])

    q, k, v, g = proj(0), proj(1), proj(2), proj(3)
    u = proj(4) * jax.nn.sigmoid(proj(5))

    for h in range(RET_HEADS):
        hs = slice(h * RET_DK, (h + 1) * RET_DK)
        dmask, din, dout, cdec = dmask_ref[h], din_ref[h], dout_ref[h], cdec_ref[h]
        for n in range(tt // RET_CHUNK):
            rs = slice(n * RET_CHUNK, (n + 1) * RET_CHUNK)
            cos2, sin2 = cos_ref[rs, :], sin_ref[rs, :]
            qc = _rope(q[rs, hs], cos2, sin2)
            kc = _rope(k[rs, hs], cos2, sin2) * (RET_DK ** -0.5)
            vc = v[rs, hs]
            r_prev = rstate[h]
            scores = _dot_nt(qc, kc) * dmask
            o = _dot(scores, vc) + _dot(qc * din, r_prev)
            kv = _dot((kc * dout).T, vc)
            rstate[h] = cdec * r_prev + kv
            mixbuf[rs, hs] = (_silu(g[rs, hs]) * _norm(o)).astype(BF16)

    ubuf[HIST_ROWS:HIST_ROWS + tt, :] = u
    cr = 64
    for r0 in range(0, tt, cr):
        acc = jnp.zeros((cr, CONV_WIDTH), F32) + dwb_ref[...]
        for kk in range(CONV_K):
            s = HIST_ROWS - (CONV_K - 1) + r0 + kk
            acc = acc + ubuf[s:s + cr, :] * dw_ref[kk:kk + 1, :]
        yc = _silu(_norm(acc) * clg_ref[...] + clb_ref[...])
        mixbuf[r0:r0 + cr, RET_WIDTH:] = yc.astype(BF16)
    ubuf[0:HIST_ROWS, :] = ubuf[tt:tt + HIST_ROWS, :]

    mix = jnp.dot(mixbuf[...], wout_ref[...], preferred_element_type=F32)
    y_ref[0] = _norm(ALPHA * x + mix) * l1g_ref[...] + l1b_ref[...]

    @pl.when(t == pl.num_programs(1) - 1)
    def _():
        r_ref[0] = rstate[...]
        cb_ref[0] = ubuf[0:HIST_ROWS, :]


def _mixer_prompt(x, win, b_in, cos2, sin2, dmask, din, dout, cdec, dw, dwb, clg, clb, wout, l1g, l1b):
    b, t, d = x.shape
    tt = min(512, t)
    assert t % tt == 0 and tt % RET_CHUNK == 0
    full = lambda a: pl.BlockSpec(a.shape, lambda i, j: (0,) * a.ndim)
    args = (x, win, b_in, cos2, sin2, dmask, din, dout, cdec, dw, dwb, clg, clb, wout, l1g, l1b)
    in_specs = [pl.BlockSpec((1, tt, d), lambda i, j: (i, j, 0)), full(win), full(b_in),
                pl.BlockSpec((tt, RET_DK), lambda i, j: (j, 0)),
                pl.BlockSpec((tt, RET_DK), lambda i, j: (j, 0))] + [full(a) for a in args[5:]]
    return pl.pallas_call(
        functools.partial(_mixer_prompt_kernel, tt=tt),
        grid=(b, t // tt),
        in_specs=in_specs,
        out_specs=[pl.BlockSpec((1, tt, d), lambda i, j: (i, j, 0)),
                   pl.BlockSpec((1, RET_HEADS, RET_DK, RET_DK), lambda i, j: (i, 0, 0, 0)),
                   pl.BlockSpec((1, HIST_ROWS, CONV_WIDTH), lambda i, j: (i, 0, 0))],
        out_shape=[jax.ShapeDtypeStruct((b, t, d), F32),
                   jax.ShapeDtypeStruct((b, RET_HEADS, RET_DK, RET_DK), F32),
                   jax.ShapeDtypeStruct((b, HIST_ROWS, CONV_WIDTH), F32)],
        scratch_shapes=[pltpu.VMEM((RET_HEADS, RET_DK, RET_DK), F32),
                        pltpu.VMEM((tt + HIST_ROWS, CONV_WIDTH), F32),
                        pltpu.VMEM((tt, D_MODEL), BF16)],
        compiler_params=pltpu.CompilerParams(dimension_semantics=("arbitrary", "arbitrary"),
                                             vmem_limit_bytes=VMEM_LIMIT),
        name="mixer_prompt",
    )(*args)


def _mixer_sample_kernel(x_ref, r0_ref, cache_ref, win_ref, bin_ref, cos_ref, sin_ref, stab_ref,
                         dw_ref, dwb_ref, clg_ref, clb_ref, wout_ref, l1g_ref, l1b_ref,
                         y_ref, r_ref, cn_ref,
                         qbuf, kbuf, mixbuf, *, nb, ns):
    rows = ns * nb
    x = x_ref[...].reshape(rows, D_MODEL)
    xb = x.astype(BF16)

    def proj(seg):
        lo = seg * RET_WIDTH
        return (jnp.dot(xb, win_ref[:, lo:lo + RET_WIDTH], preferred_element_type=F32)
                + bin_ref[:, lo:lo + RET_WIDTH])

    q, k, v, g = proj(0), proj(1), proj(2), proj(3)
    u = proj(4) * jax.nn.sigmoid(proj(5))
    row_b = lax.broadcasted_iota(jnp.int32, (rows, RET_DK), 0) % nb

    for h in range(RET_HEADS):
        hs = slice(h * RET_DK, (h + 1) * RET_DK)
        qs, ks, vs = [], [], []
        for j in range(ns):
            rs = slice(j * nb, (j + 1) * nb)
            cos2, sin2 = cos_ref[j:j + 1, :], sin_ref[j:j + 1, :]
            qj = _rope(q[rs, hs], cos2, sin2)
            kj = _rope(k[rs, hs], cos2, sin2) * (RET_DK ** -0.5)
            qs.append(qj)
            ks.append(kj)
            vs.append(v[rs, hs])
            qbuf[rs, :] = qj * stab_ref[h, j:j + 1, :]
            kbuf[rs, :] = kj * stab_ref[h, 4 + j:5 + j, :]
        qd = qbuf[...]
        kdt = kbuf[...].T
        vh = v[:, hs]
        cdec = stab_ref[h, 8:9, :]
        o_cross = jnp.zeros((rows, RET_DK), F32)
        for b in range(nb):
            sel = row_b == b
            r_prev = r0_ref[b, h]
            o_cross = o_cross + _dot(jnp.where(sel, qd, 0.0), r_prev)
            r_ref[b, h] = cdec * r_prev + _dot(kdt, jnp.where(sel, vh, 0.0))
        for i in range(ns):
            rs = slice(i * nb, (i + 1) * nb)
            o = o_cross[rs, :]
            for j in range(i + 1):
                sc = jnp.sum(qs[i] * ks[j], axis=-1, keepdims=True)
                o = o + (sc * stab_ref[h, 16 + 4 * i + j:17 + 4 * i + j, :]) * vs[j]
            mixbuf[rs, hs] = (_silu(g[rs, hs]) * _norm(o)).astype(BF16)

    nc = CONV_K - 1
    us = [u[j * nb:(j + 1) * nb, :] for j in range(ns)]
    for t in range(ns):
        acc = jnp.zeros((nb, CONV_WIDTH), F32) + dwb_ref[...]
        for r in range(t, nc):
            acc = acc + cache_ref[:, r * CONV_WIDTH:(r + 1) * CONV_WIDTH] * dw_ref[r - t:r - t + 1, :]
        for j in range(t + 1):
            kk = nc - t + j
            acc = acc + us[j] * dw_ref[kk:kk + 1, :]
        yc = _silu(_norm(acc) * clg_ref[...] + clb_ref[...])
        mixbuf[t * nb:(t + 1) * nb, RET_WIDTH:] = yc.astype(BF16)
    cn_ref[:, 0:(nc - ns) * CONV_WIDTH] = cache_ref[:, ns * CONV_WIDTH:nc * CONV_WIDTH]
    for j in range(ns):
        lo = (nc - ns + j) * CONV_WIDTH
        cn_ref[:, lo:lo + CONV_WIDTH] = us[j]

    mix = jnp.dot(mixbuf[...], wout_ref[...], preferred_element_type=F32)
    y = _norm(ALPHA * x + mix) * l1g_ref[...] + l1b_ref[...]
    y_ref[...] = y.reshape(ns, nb, D_MODEL)


def _mixer_sample(x_tm, r0, cache_flat, win, b_in, cos2, sin2, stab, dw, dwb, clg, clb, wout, l1g, l1b):
    ns, nbatch, d = x_tm.shape
    nb = 16
    assert nbatch % nb == 0 and ns == 4
    full = lambda a: pl.BlockSpec(a.shape, lambda i: (0,) * a.ndim)
    args = (x_tm, r0, cache_flat, win, b_in, cos2, sin2, stab, dw, dwb, clg, clb, wout, l1g, l1b)
    cw = cache_flat.shape[1]
    in_specs = [pl.BlockSpec((ns, nb, d), lambda i: (0, i, 0)),
                pl.BlockSpec((nb, RET_HEADS, RET_DK, RET_DK), lambda i: (i, 0, 0, 0)),
                pl.BlockSpec((nb, cw), lambda i: (i, 0))] + [full(a) for a in args[3:]]
    return pl.pallas_call(
        functools.partial(_mixer_sample_kernel, nb=nb, ns=ns),
        grid=(nbatch // nb,),
        in_specs=in_specs,
        out_specs=[pl.BlockSpec((ns, nb, d), lambda i: (0, i, 0)),
                   pl.BlockSpec((nb, RET_HEADS, RET_DK, RET_DK), lambda i: (i, 0, 0, 0)),
                   pl.BlockSpec((nb, cw), lambda i: (i, 0))],
        out_shape=[jax.ShapeDtypeStruct((ns, nbatch, d), F32),
                   jax.ShapeDtypeStruct(r0.shape, F32),
                   jax.ShapeDtypeStruct(cache_flat.shape, F32)],
        scratch_shapes=[pltpu.VMEM((ns * nb, RET_DK), F32),
                        pltpu.VMEM((ns * nb, RET_DK), F32),
                        pltpu.VMEM((ns * nb, D_MODEL), BF16)],
        compiler_params=pltpu.CompilerParams(dimension_semantics=("arbitrary",),
                                             vmem_limit_bytes=VMEM_LIMIT),
        name="mixer_sample",
    )(*args)


PEER_ROWS = 32
NEG_INF = float("-inf")
NO_RANK = float(PEER_TOPK)


def _top16(s, vs_ref):
    work = s
    rank = jnp.full(s.shape, NO_RANK, F32)
    for r in range(PEER_TOPK):
        m = jnp.max(work, axis=0, keepdims=True)
        vs_ref[r:r + 1, :] = m
        hit = work == m
        rank = jnp.where(hit, float(r), rank)
        if r + 1 < PEER_TOPK:
            work = jnp.where(hit, NEG_INF, work)
    return rank


def _peer_kernel(x_ref, wq_ref, sk1_ref, sk2_ref, u_ref, vt_ref, l2g_ref, l2b_ref, y_ref,
                 xb_ref, n1_ref, w1_ref, r2_ref, p2_ref, v1_ref, v2_ref, cand_ref,
                 h_ref, ga_ref, acc_ref, *, tm, te):
    j = pl.program_id(1)
    e1_per_blk = te // N_KEYS

    @pl.when(j == 0)
    def _():
        xb = x_ref[...].astype(BF16)
        xb_ref[...] = xb
        q = jnp.dot(xb, wq_ref[...], preferred_element_type=F32)
        acc_ref[...] = jnp.zeros_like(acc_ref)
        half = PEER_DQ // 2
        for h in range(PEER_HEADS):
            s1 = _dot_nt(sk1_ref[h], q[:, h * PEER_DQ:h * PEER_DQ + half])
            s2 = _dot_nt(sk2_ref[h], q[:, h * PEER_DQ + half:(h + 1) * PEER_DQ])
            rank1 = _top16(s1, v1_ref)
            rank2 = _top16(s2, v2_ref)
            cand_ref[0:PEER_TOPK, :] = v1_ref[0:1, :] + v2_ref[...]
            for i in range(1, PEER_TOPK):
                lo = PEER_TOPK + 8 * (i - 1)
                cand_ref[lo:lo + 8, :] = v1_ref[i:i + 1, :] + v2_ref[0:8, :]
            work = cand_ref[...]
            top = v1_ref[0:1, :] + v2_ref[0:1, :]
            z = jnp.zeros_like(top)
            for r in range(PEER_TOPK):
                m = jnp.max(work, axis=0, keepdims=True)
                z = z + jnp.exp(m - top)
                if r + 1 < PEER_TOPK:
                    work = jnp.where(work == m, NEG_INF, work)
            taken = jnp.where(cand_ref[...] >= m, 1.0, 0.0)
            n1 = jnp.zeros_like(s1)
            for i in range(PEER_TOPK):
                lo, cnt = (0, PEER_TOPK) if i == 0 else (PEER_TOPK + 8 * (i - 1), 8)
                n_i = jnp.sum(taken[lo:lo + cnt, :], axis=0, keepdims=True)
                n1 = jnp.where(rank1 == float(i), n_i, n1)
            n1_ref[h] = n1
            w1_ref[h] = jnp.exp(s1 - v1_ref[0:1, :]) / z
            r2_ref[h] = rank2.astype(BF16)
            p2_ref[h] = jnp.exp(s2 - v2_ref[0:1, :]).astype(BF16)

    h_ref[...] = _dot_nt(u_ref[...], xb_ref[...])

    def body(it, carry):
        r0 = pl.multiple_of(it * PEER_ROWS, PEER_ROWS)
        e1 = j * e1_per_blk + it // (N_KEYS // PEER_ROWS)
        k0 = pl.multiple_of((it % (N_KEYS // PEER_ROWS)) * PEER_ROWS, PEER_ROWS)
        gate = jnp.zeros((PEER_ROWS, tm), BF16)
        for h in range(PEER_HEADS):
            n = n1_ref[h, pl.ds(e1, 1), :].astype(BF16)
            w = w1_ref[h, pl.ds(e1, 1), :].astype(BF16)
            r2 = r2_ref[h, pl.ds(k0, PEER_ROWS), :]
            p2 = p2_ref[h, pl.ds(k0, PEER_ROWS), :]
            gate = gate + jnp.where(r2 < n, p2 * w, jnp.zeros_like(p2))
        hv = h_ref[pl.ds(r0, PEER_ROWS), :]
        act = 0.5 * hv * (1.0 + lax.erf(hv * (2.0 ** -0.5)))
        ga_ref[pl.ds(r0, PEER_ROWS), :] = gate * act.astype(BF16)
        return carry

    lax.fori_loop(0, te // PEER_ROWS, body, 0)
    acc_ref[...] += jnp.dot(vt_ref[...], ga_ref[...], preferred_element_type=F32)

    @pl.when(j == pl.num_programs(1) - 1)
    def _():
        y = ALPHA * x_ref[...] + acc_ref[...].T
        y_ref[...] = _norm(y) * l2g_ref[...] + l2b_ref[...]


def _peer(x, wq, sk1, sk2, u_b, vt_b, l2g, l2b):
    n, d = x.shape
    ne = u_b.shape[0]
    tm = min(512, n)
    te = 512
    assert n % tm == 0 and ne % te == 0 and te % N_KEYS == 0
    full = lambda a: pl.BlockSpec(a.shape, lambda i, j: (0,) * a.ndim)
    hk = (PEER_HEADS, N_KEYS, tm)
    ncand = PEER_TOPK + 8 * (PEER_TOPK - 1)
    return pl.pallas_call(
        functools.partial(_peer_kernel, tm=tm, te=te),
        grid=(n // tm, ne // te),
        in_specs=[pl.BlockSpec((tm, d), lambda i, j: (i, 0)), full(wq), full(sk1), full(sk2),
                  pl.BlockSpec((te, d), lambda i, j: (j, 0)),
                  pl.BlockSpec((d, te), lambda i, j: (0, j)), full(l2g), full(l2b)],
        out_specs=pl.BlockSpec((tm, d), lambda i, j: (i, 0)),
        out_shape=jax.ShapeDtypeStruct((n, d), F32),
        scratch_shapes=[pltpu.VMEM((tm, d), BF16),
                        pltpu.VMEM(hk, F32), pltpu.VMEM(hk, F32), pltpu.VMEM(hk, BF16), pltpu.VMEM(hk, BF16),
                        pltpu.VMEM((PEER_TOPK, tm), F32), pltpu.VMEM((PEER_TOPK, tm), F32),
                        pltpu.VMEM((ncand, tm), F32),
                        pltpu.VMEM((te, tm), F32), pltpu.VMEM((te, tm), BF16),
                        pltpu.VMEM((d, tm), F32)],
        compiler_params=pltpu.CompilerParams(dimension_semantics=("arbitrary", "arbitrary"),
                                             vmem_limit_bytes=VMEM_LIMIT),
        name="peer",
    )(x, wq, sk1, sk2, u_b, vt_b, l2g, l2b)


def _rope_tables(pos):
    inv_freq = ROPE_BASE ** (-jnp.arange(0, RET_DK, 2, dtype=F32) / RET_DK)
    ang = pos.astype(F32)[:, None] * inv_freq[None, :]
    cos, sin = jnp.cos(ang), jnp.sin(ang)
    return jnp.concatenate([cos, cos], axis=-1), jnp.concatenate([-sin, sin], axis=-1)


def _decay_tables(log_gamma, c):
    i = jnp.arange(c, dtype=F32)
    rel = i[:, None] - i[None, :]
    lg = log_gamma[:, None, None]
    dmask = jnp.where(rel >= 0, jnp.exp(jnp.maximum(rel, 0.0) * lg), 0.0)
    decay_in = jnp.exp((i[None, :] + 1.0) * log_gamma[:, None])
    decay_out = jnp.exp((c - 1.0 - i[None, :]) * log_gamma[:, None])
    chunk_decay = jnp.exp(c * log_gamma)
    return dmask, decay_in, decay_out, chunk_decay


def kernel(x_prompt, x_sample, state_retention, cache_conv, w_in, b_in, dw_kernel, dw_bias, conv_ln_g,
           conv_ln_b, w_out, ln1_g, ln1_b, w_query, sub_keys_1, sub_keys_2, peer_u, peer_v, ln2_g, ln2_b):
    bp, tp, d = x_prompt.shape
    bs, ns, _ = x_sample.shape
    log_gamma = jnp.log(1.0 - 2.0 ** (-5.0 - jnp.arange(RET_HEADS, dtype=F32)))
    cos_p, sin_p = _rope_tables(jnp.arange(tp))
    cos_s, sin_s = _rope_tables(PAST_LEN + jnp.arange(ns))

    cp = min(tp, RET_CHUNK)
    dmask, d_in, d_out, c_dec = _decay_tables(log_gamma, cp)
    rep = lambda a: jnp.broadcast_to(a[..., None], a.shape + (LANES,))
    din_t, dout_t = rep(d_in), rep(d_out)
    cdec_t = jnp.broadcast_to(c_dec[:, None, None], (RET_HEADS, cp, LANES))

    dm4, di4, do4, cd4 = _decay_tables(log_gamma, ns)
    stab = jnp.concatenate([di4, do4, cd4[:, None], jnp.zeros((RET_HEADS, 7), F32),
                            dm4.reshape(RET_HEADS, ns * ns)], axis=1)
    stab = rep(stab)

    yp = x_prompt
    ys = jnp.transpose(x_sample, (1, 0, 2))
    row = lambda a: a.reshape(1, -1)
    ret_p, conv_p, ret_s, conv_s = [], [], [], []
    for l in range(DEPTH):
        win, wout, wq = w_in[l].astype(BF16), w_out[l].astype(BF16), w_query[l].astype(BF16)
        mix_w = (row(b_in[l]),)
        conv_w = (dw_kernel[l], row(dw_bias[l]), row(conv_ln_g[l]), row(conv_ln_b[l]), wout,
                  row(ln1_g[l]), row(ln1_b[l]))
        peer_w = (wq, sub_keys_1[l].astype(BF16), sub_keys_2[l].astype(BF16),
                  peer_u[l].astype(BF16), peer_v[l].T.astype(BF16), row(ln2_g[l]), row(ln2_b[l]))

        xp, rp, cbp = _mixer_prompt(yp, win, *mix_w, cos_p, sin_p, dmask, din_t, dout_t, cdec_t, *conv_w)
        xs, rs, cs = _mixer_sample(ys, state_retention[l], cache_conv[l].reshape(bs, -1), win, *mix_w,
                                   cos_s, sin_s, stab, *conv_w)
        yp = _peer(xp.reshape(bp * tp, d), *peer_w).reshape(bp, tp, d)
        ys = _peer(xs.reshape(ns * bs, d), *peer_w).reshape(ns, bs, d)
        ret_p.append(rp)
        conv_p.append(cbp[:, HIST_ROWS - (CONV_K - 1):])
        ret_s.append(rs)
        conv_s.append(cs.reshape(bs, CONV_K - 1, CONV_WIDTH))
    return (yp, jnp.transpose(ys, (1, 0, 2)), jnp.stack(ret_p), jnp.stack(conv_p),
            jnp.stack(ret_s), jnp.stack(conv_s))
```
